```python
import jax, jax.numpy as jnp
from jax import lax
import numpy as np

D_MODEL = 1024
BATCH = 8
SEQ = 2048
DEPTH = 4

CTX_LEN = 256
GRID_W = 64
N_MIXERS = 3
EPS = 1e-6
CONV_WIDTH = 31
HEAD_DIM = 64
N_Q_HEADS = D_MODEL // HEAD_DIM
N_KV_HEADS = N_Q_HEADS // 4
Q_PER_KV = N_Q_HEADS // N_KV_HEADS
WINDOW = 128
ATTN_BLOCK = 128
ROPE_BASE = 10000.0
GMLP_CHUNK = 128
GMLP_WIDTH = 2 * D_MODEL
GMLP_GROUP_DIM = 128
GMLP_GROUPS = GMLP_WIDTH // GMLP_GROUP_DIM
FFN_DIM = 2816
FFN_CONV_WIDTH = 3

kernel_name = "hybrid_interleaved_conv_swa_gmlp_dit"


def rms_norm(x, g):
    xf = x.astype(jnp.float32)
    y = xf * lax.rsqrt(jnp.mean(xf * xf, axis=-1, keepdims=True) + EPS)
    return (y * g.astype(jnp.float32)).astype(x.dtype)


def layer_norm(x, g, b):
    xf = x.astype(jnp.float32)
    mu = jnp.mean(xf, axis=-1, keepdims=True)
    var = jnp.mean(jnp.square(xf - mu), axis=-1, keepdims=True)
    y = (xf - mu) * lax.rsqrt(var + EPS)
    return (y * g.astype(jnp.float32) + b.astype(jnp.float32)).astype(x.dtype)


def depthwise_conv(x, w, b):
    k = w.shape[0]
    half = (k - 1) // 2
    y = lax.conv_general_dilated(
        x, w[:, None, :].astype(x.dtype), window_strides=(1,), padding=[(half, half)],
        dimension_numbers=("NWC", "WIO", "NWC"), feature_group_count=x.shape[-1])
    return y + b


def axial_rope_tables(rows):
    row = jnp.repeat(jnp.arange(rows), GRID_W).astype(jnp.float32)
    col = jnp.tile(jnp.arange(GRID_W), rows).astype(jnp.float32)
    axis_dim = HEAD_DIM // 2
    inv_freq = ROPE_BASE ** (-jnp.arange(0, axis_dim, 2, dtype=jnp.float32) / axis_dim)
    ang_r = row[:, None] * inv_freq[None, :]
    ang_c = col[:, None] * inv_freq[None, :]
    ang = jnp.concatenate([ang_r, ang_r, ang_c, ang_c], axis=-1)
    return jnp.cos(ang), jnp.sin(ang)


def rotate_half(x):
    x1, x2 = jnp.split(x, 2, axis=-1)
    return jnp.concatenate([-x2, x1], axis=-1)


def apply_axial_rope(x, cos, sin):
    xr, xc = jnp.split(x, 2, axis=-1)
    rot = jnp.concatenate([rotate_half(xr), rotate_half(xc)], axis=-1)
    out = x * cos[None, :, None, :] + rot * sin[None, :, None, :]
    return out.astype(x.dtype)


def adaln_params(cond, w, b):
    mod = (jax.nn.silu(cond) @ w + b)[:, None, :]
    return jnp.split(mod, 6, axis=-1)


def conformer_conv(h, w_in, b_in, dw_w, dw_b, ln_g, ln_b, w_out, b_out):
    a, gt = jnp.split(h @ w_in + b_in, 2, axis=-1)
    z = a * jax.nn.sigmoid(gt)
    z = depthwise_conv(z, dw_w, dw_b)
    z = jax.nn.silu(layer_norm(z, ln_g, ln_b))
    return z @ w_out + b_out


def chunk_gmlp(h, w_in, b_in, ln_g, ln_b, w_s, b_s, w_out):
    bsz, length, _ = h.shape
    u, v = jnp.split(jax.nn.gelu(h @ w_in + b_in), 2, axis=-1)
    v = layer_norm(v, ln_g, ln_b)
    v = v.reshape(bsz, length // GMLP_CHUNK, GMLP_CHUNK, GMLP_GROUPS, GMLP_GROUP_DIM)
    s = jnp.einsum("gpq,bnqgc->bnpgc", w_s, v) + b_s.T[None, None, :, :, None]
    s = s.reshape(bsz, length, GMLP_WIDTH)
    return (u * s) @ w_out


def windowed_gqa(h, hc, w_qkv, sink, w_o, cos, sin, ctx_out):
    bsz, length, _ = h.shape
    n_ctx = hc.shape[1]
    nb = length // ATTN_BLOCK
    q_dim = N_Q_HEADS * HEAD_DIM
    kv_dim = N_KV_HEADS * HEAD_DIM
    scale = HEAD_DIM ** -0.5
    neg = jnp.float32(-1e30)

    q, k, v = jnp.split(h @ w_qkv, [q_dim, q_dim + kv_dim], axis=-1)
    q = apply_axial_rope(q.reshape(bsz, length, N_Q_HEADS, HEAD_DIM), cos, sin)
    k = apply_axial_rope(k.reshape(bsz, length, N_KV_HEADS, HEAD_DIM), cos, sin)
    v = v.reshape(bsz, length, N_KV_HEADS, HEAD_DIM)
    if ctx_out:
        qc, kc, vc = jnp.split(hc @ w_qkv, [q_dim, q_dim + kv_dim], axis=-1)
    else:
        kc, vc = jnp.split(hc @ w_qkv[:, q_dim:], 2, axis=-1)
    kc = kc.reshape(bsz, n_ctx, N_KV_HEADS, HEAD_DIM)
    vc = vc.reshape(bsz, n_ctx, N_KV_HEADS, HEAD_DIM)
    sink_f = sink.astype(jnp.float32).reshape(N_KV_HEADS, Q_PER_KV)

    pad = ((0, 0), (ATTN_BLOCK, ATTN_BLOCK), (0, 0), (0, 0))
    kp = jnp.pad(k, pad).reshape(bsz, nb + 2, ATTN_BLOCK, N_KV_HEADS, HEAD_DIM)
    vp = jnp.pad(v, pad).reshape(bsz, nb + 2, ATTN_BLOCK, N_KV_HEADS, HEAD_DIM)
    band = lambda t: jnp.concatenate([t[:, :-2], t[:, 1:-1], t[:, 2:]], axis=2)
    kw, vw = band(kp), band(vp)
    qb = q.reshape(bsz, nb, ATTN_BLOCK, N_KV_HEADS, Q_PER_KV, HEAD_DIM)

    s_win = jnp.einsum("bnqhgd,bnkhd->bnhgqk", qb, kw).astype(jnp.float32) * scale
    s_ctx = jnp.einsum("bnqhgd,bkhd->bnhgqk", qb, kc).astype(jnp.float32) * scale
    q_off = jnp.arange(ATTN_BLOCK)[:, None] + ATTN_BLOCK
    k_off = jnp.arange(3 * ATTN_BLOCK)[None, :]
    in_window = jnp.abs(q_off - k_off) <= WINDOW
    key_abs = (jnp.arange(nb)[:, None] - 1) * ATTN_BLOCK + jnp.arange(3 * ATTN_BLOCK)[None, :]
    valid = (key_abs >= 0) & (key_abs < length)
    mask = in_window[None] & valid[:, None, :]
    s_win = jnp.where(mask[None, :, None, None], s_win, neg)

    sk = sink_f[None, None, :, :, None, None]
    m = jnp.maximum(jnp.maximum(s_win.max(-1, keepdims=True), s_ctx.max(-1, keepdims=True)), sk)
    p_win = jnp.exp(s_win - m)
    p_ctx = jnp.exp(s_ctx - m)
    inv = 1.0 / (p_win.sum(-1, keepdims=True) + p_ctx.sum(-1, keepdims=True) + jnp.exp(sk - m))
    o = (jnp.einsum("bnhgqk,bnkhd->bnqhgd", p_win * inv, vw)
         + jnp.einsum("bnhgqk,bkhd->bnqhgd", p_ctx * inv, vc))
    y = o.astype(h.dtype).reshape(bsz, length, q_dim) @ w_o

    if not ctx_out:
        return y, None
    qc = qc.reshape(bsz, n_ctx, N_KV_HEADS, Q_PER_KV, HEAD_DIM)
    sc = jnp.einsum("bqhgd,bkhd->bhgqk", qc, kc).astype(jnp.float32) * scale
    skc = sink_f[None, :, :, None, None]
    mc = jnp.maximum(sc.max(-1, keepdims=True), skc)
    pc = jnp.exp(sc - mc)
    pc = pc / (pc.sum(-1, keepdims=True) + jnp.exp(skc - mc))
    oc = jnp.einsum("bhgqk,bkhd->bqhgd", pc, vc).astype(hc.dtype)
    yc = oc.reshape(bsz, n_ctx, q_dim) @ w_o
    return y, yc


def conv_ffn(h, w_up, conv_w, conv_b, w_down):
    z = depthwise_conv(h @ w_up, conv_w, conv_b)
    gate, val = jnp.split(z, 2, axis=-1)
    return (jax.nn.silu(gate) * val) @ w_down


def setup_inputs(seed: int = 0) -> dict:
    key = jax.random.key(seed)
    keys = jax.random.split(key, 32)
    ks = [keys[i] for i in range(32)]
    counter = [0]

    def nrm(shape, s):
        k = ks[counter[0]]
        counter[0] += 1
        return jax.random.normal(k, shape, jnp.float32) * s

    D = D_MODEL
    n_a = len(range(0, DEPTH, N_MIXERS))
    n_b = len(range(1, DEPTH, N_MIXERS))
    n_c = len(range(2, DEPTH, N_MIXERS))
    qkv_dim = (N_Q_HEADS + 2 * N_KV_HEADS) * HEAD_DIM
    return {
        "x": nrm((BATCH, SEQ, D), 1.0),
        "c": nrm((BATCH, D), 1.0),
        "ctx": nrm((BATCH, CTX_LEN, D), 1.0),
        "c_ctx": nrm((D,), 1.0),
        "ada_w": nrm((DEPTH, D, 6 * D), 0.5 * D ** -0.5),
        "ada_b": nrm((DEPTH, 6 * D), 0.02),
        "norm_g": 1.0 + nrm((DEPTH, 4, D), 0.02),
        "ffn_w_up": nrm((DEPTH, D, 2 * FFN_DIM), D ** -0.5),
        "ffn_conv_w": nrm((DEPTH, FFN_CONV_WIDTH, 2 * FFN_DIM), FFN_CONV_WIDTH ** -0.5),
        "ffn_conv_b": nrm((DEPTH, 2 * FFN_DIM), 0.02),
        "ffn_w_down": nrm((DEPTH, FFN_DIM, D), FFN_DIM ** -0.5),
        "cm_w_in": nrm((n_a, D, 2 * D), D ** -0.5),
        "cm_b_in": nrm((n_a, 2 * D), 0.02),
        "cm_dw_w": nrm((n_a, CONV_WIDTH, D), CONV_WIDTH ** -0.5),
        "cm_dw_b": nrm((n_a, D), 0.02),
        "cm_ln_g": 1.0 + nrm((n_a, D), 0.02),
        "cm_ln_b": nrm((n_a, D), 0.02),
        "cm_w_out": nrm((n_a, D, D), D ** -0.5),
        "cm_b_out": nrm((n_a, D), 0.02),
        "attn_w_qkv": nrm((n_b, D, qkv_dim), D ** -0.5),
        "attn_sink": nrm((n_b, N_Q_HEADS), 0.5),
        "attn_w_o": nrm((n_b, N_Q_HEADS * HEAD_DIM, D), D ** -0.5),
        "gm_w_in": nrm((n_c, D, 2 * GMLP_WIDTH), D ** -0.5),
        "gm_b_in": nrm((n_c, 2 * GMLP_WIDTH), 0.02),
        "gm_ln_g": 1.0 + nrm((n_c, GMLP_WIDTH), 0.02),
        "gm_ln_b": nrm((n_c, GMLP_WIDTH), 0.02),
        "gm_w_s": nrm((n_c, GMLP_GROUPS, GMLP_CHUNK, GMLP_CHUNK), GMLP_CHUNK ** -0.5),
        "gm_b_s": 1.0 + nrm((n_c, GMLP_GROUPS, GMLP_CHUNK), 0.02),
        "gm_w_out": nrm((n_c, GMLP_WIDTH, D), GMLP_WIDTH ** -0.5),
    }


def reference(x, c, ctx, c_ctx, ada_w, ada_b, norm_g, ffn_w_up, ffn_conv_w, ffn_conv_b, ffn_w_down,
              cm_w_in, cm_b_in, cm_dw_w, cm_dw_b, cm_ln_g, cm_ln_b, cm_w_out, cm_b_out,
              attn_w_qkv, attn_sink, attn_w_o,
              gm_w_in, gm_b_in, gm_ln_g, gm_ln_b, gm_w_s, gm_b_s, gm_w_out):
    length = x.shape[1]
    ROWS = length // GRID_W
    cos, sin = axial_rope_tables(ROWS)
    ctx_readers = [i for i in range(DEPTH) if i % N_MIXERS == 1]
    last_reader = max(ctx_readers) if ctx_readers else -1

    h, hc = x, ctx
    for i in range(DEPTH):
        kind, j = i % N_MIXERS, i // N_MIXERS
        use_ctx = i <= last_reader
        ctx_full = i < last_reader

        sh1, sc1, g1, sh2, sc2, g2 = adaln_params(c, ada_w[i], ada_b[i])
        a = rms_norm(h, norm_g[i, 0]) * (1.0 + sc1) + sh1
        if use_ctx:
            csh1, csc1, cg1, csh2, csc2, cg2 = adaln_params(c_ctx[None, :], ada_w[i], ada_b[i])
            ac = rms_norm(hc, norm_g[i, 0]) * (1.0 + csc1) + csh1

        if kind == 0:
            cm = (cm_w_in[j], cm_b_in[j], cm_dw_w[j], cm_dw_b[j], cm_ln_g[j], cm_ln_b[j], cm_w_out[j], cm_b_out[j])
            y = conformer_conv(a, *cm)
            yc = conformer_conv(ac, *cm) if ctx_full else None
        elif kind == 1:
            y, yc = windowed_gqa(a, ac, attn_w_qkv[j], attn_sink[j], attn_w_o[j], cos, sin, ctx_full)
        else:
            gm = (gm_w_in[j], gm_b_in[j], gm_ln_g[j], gm_ln_b[j], gm_w_s[j], gm_b_s[j], gm_w_out[j])
            y = chunk_gmlp(a, *gm)
            yc = chunk_gmlp(ac, *gm) if ctx_full else None

        h = h + g1 * rms_norm(y, norm_g[i, 1])
        f = conv_ffn(rms_norm(h, norm_g[i, 2]) * (1.0 + sc2) + sh2,
                     ffn_w_up[i], ffn_conv_w[i], ffn_conv_b[i], ffn_w_down[i])
        h = h + g2 * rms_norm(f, norm_g[i, 3])

        if ctx_full:
            hc = hc + cg1 * rms_norm(yc, norm_g[i, 1])
            fc = conv_ffn(rms_norm(hc, norm_g[i, 2]) * (1.0 + csc2) + csh2,
                          ffn_w_up[i], ffn_conv_w[i], ffn_conv_b[i], ffn_w_down[i])
            hc = hc + cg2 * rms_norm(fc, norm_g[i, 3])
    return h
```

```python
import functools

import jax
import jax.numpy as jnp
from jax import lax
from jax.experimental import pallas as pl
from jax.experimental.pallas import tpu as pltpu

D_MODEL = 1024
DEPTH = 4
GRID_W = 64
N_MIXERS = 3
EPS = 1e-6
CONV_WIDTH = 31
HEAD_DIM = 64
N_Q_HEADS = D_MODEL // HEAD_DIM
N_KV_HEADS = N_Q_HEADS // 4
ATTN_BLOCK = 128
ROPE_BASE = 10000.0
GMLP_CHUNK = 128
GMLP_WIDTH = 2 * D_MODEL
GMLP_GROUP_DIM = 128
GMLP_GROUPS = GMLP_WIDTH // GMLP_GROUP_DIM
FFN_DIM = 2816

LANES = 128
HALO = 16
FFN_CHUNK = 256
FFN_NCHUNK = FFN_DIM // FFN_CHUNK
GMLP_PAIR = 2 * GMLP_GROUP_DIM
GMLP_NPAIR = GMLP_WIDTH // GMLP_PAIR
KV_DIM = N_KV_HEADS * HEAD_DIM
KV_EXP = N_KV_HEADS * 2 * LANES
VMEM_LIMIT = 56 * 1024 * 1024

BF16 = jnp.bfloat16
F32 = jnp.float32


def _mm(a, b):
    return jnp.dot(a, b, preferred_element_type=F32)


def _mm_nt(a, b):
    return lax.dot_general(a, b, (((1,), (1,)), ((), ())), preferred_element_type=F32)


def _rms(x, g):
    return x * lax.rsqrt(jnp.mean(x * x, axis=-1, keepdims=True) + EPS) * g


def _layer_norm(x, g, b):
    mu = jnp.mean(x, axis=-1, keepdims=True)
    xc = x - mu
    var = jnp.mean(xc * xc, axis=-1, keepdims=True)
    return xc * lax.rsqrt(var + EPS) * g + b


def _sigmoid(x):
    return 1.0 / (1.0 + jnp.exp(-x))


def _gelu_tanh(x):
    return 0.5 * x * (1.0 + jnp.tanh(0.7978845608028654 * (x + 0.044715 * (x * x * x))))


def _mod_rows(mod_ref, first):
    return (mod_ref[0, first:first + 1, :], mod_ref[0, first + 1:first + 2, :],
            mod_ref[0, first + 2:first + 3, :])


def _fill_modnorm(a_scr, hp_ref, h_ref, hn_ref, g, sc, sh, tm):
    def f(x):
        return (_rms(x, g) * (1.0 + sc) + sh).astype(BF16)
    a_scr[0:HALO, :] = f(hp_ref[0])
    a_scr[HALO:HALO + tm, :] = f(h_ref[0])
    a_scr[HALO + tm:HALO + tm + HALO, :] = f(hn_ref[0])


def _ada_kernel(cond_ref, w_ref, b_ref, o_ref):
    c = cond_ref[...]
    s = (c * _sigmoid(c)).astype(BF16)
    o_ref[0] = _mm(s, w_ref[0].astype(BF16)) + b_ref[0]


def _ada_call(cond, ada_w, ada_b):
    rows = cond.shape[0]
    n_out = ada_w.shape[-1]
    tn = 1536
    return pl.pallas_call(
        _ada_kernel,
        grid=(DEPTH, n_out // tn),
        in_specs=[
            pl.BlockSpec((rows, D_MODEL), lambda i, n: (0, 0)),
            pl.BlockSpec((1, D_MODEL, tn), lambda i, n: (i, 0, n)),
            pl.BlockSpec((1, 1, tn), lambda i, n: (i, 0, n)),
        ],
        out_specs=pl.BlockSpec((1, rows, tn), lambda i, n: (i, 0, n)),
        out_shape=jax.ShapeDtypeStruct((DEPTH, rows, n_out), F32),
        compiler_params=pltpu.CompilerParams(
            dimension_semantics=("arbitrary", "arbitrary"), vmem_limit_bytes=VMEM_LIMIT),
        name="adaln",
    )(cond, ada_w, ada_b.reshape(DEPTH, 1, n_out))


def _const_spec(shape):
    nd = len(shape)
    return pl.BlockSpec(shape, lambda b, t: (0,) * nd, pipeline_mode=pl.Buffered(1))


def _hidden_specs(tm, seq_len):
    per = tm // HALO
    last = seq_len // HALO - 1
    return [
        pl.BlockSpec((1, HALO, D_MODEL), lambda b, t: (b, jnp.maximum(t * per - 1, 0), 0)),
        pl.BlockSpec((1, tm, D_MODEL), lambda b, t: (b, t, 0)),
        pl.BlockSpec((1, HALO, D_MODEL), lambda b, t: (b, jnp.minimum((t + 1) * per, last), 0)),
    ]


def _mod_spec(mod):
    if mod.shape[0] == 1:
        return pl.BlockSpec((1, 6, D_MODEL), lambda b, t: (0, 0, 0))
    return pl.BlockSpec((1, 6, D_MODEL), lambda b, t: (b, 0, 0))


def _params():
    return pltpu.CompilerParams(dimension_semantics=("arbitrary", "arbitrary"),
                                vmem_limit_bytes=VMEM_LIMIT)


def _tile_rows(seq_len):
    return min(512, seq_len)


def _ffn_kernel(hp_ref, h_ref, hn_ref, mod_ref, g_ref, wup_ref, cw_ref, cb_ref, wdn_ref, o_ref,
                a_scr, acc_scr, *, tm):
    t = pl.program_id(1)
    nt = pl.num_programs(1)
    sh, sc, gate = _mod_rows(mod_ref, 3)
    _fill_modnorm(a_scr, hp_ref, h_ref, hn_ref, g_ref[2:3, :], sc, sh, tm)

    @pl.when(t == 0)
    def _():
        a_scr[0:HALO, :] = jnp.zeros((HALO, D_MODEL), BF16)

    @pl.when(t == nt - 1)
    def _():
        a_scr[HALO + tm:HALO + tm + HALO, :] = jnp.zeros((HALO, D_MODEL), BF16)

    acc_scr[...] = jnp.zeros_like(acc_scr)
    rows = tm + 2 * HALO

    def conv3(z, w, b):
        zm = pltpu.roll(z, 1, 0)[HALO:HALO + tm]
        zp = pltpu.roll(z, rows - 1, 0)[HALO:HALO + tm]
        return zm * w[0:1] + z[HALO:HALO + tm] * w[1:2] + zp * w[2:3] + b

    def chunk(j, carry):
        a = a_scr[...]
        yg = conv3(_mm(a, wup_ref[0, j]), cw_ref[0, j], cb_ref[0, j])
        yv = conv3(_mm(a, wup_ref[1, j]), cw_ref[1, j], cb_ref[1, j])
        act = (yg * _sigmoid(yg) * yv).astype(BF16)
        acc_scr[...] += _mm(act, wdn_ref[j])
        return carry

    lax.fori_loop(0, FFN_NCHUNK, chunk, 0)
    o_ref[0] = h_ref[0] + gate * _rms(acc_scr[...], g_ref[3:4, :])


def _ffn_call(h, mod, norm_g, w_up, conv_w, conv_b, w_down):
    bsz, seq_len, _ = h.shape
    tm = _tile_rows(seq_len)
    return pl.pallas_call(
        functools.partial(_ffn_kernel, tm=tm),
        grid=(bsz, seq_len // tm),
        in_specs=_hidden_specs(tm, seq_len) + [
            _mod_spec(mod),
            _const_spec(norm_g.shape),
            _const_spec(w_up.shape),
            _const_spec(conv_w.shape),
            _const_spec(conv_b.shape),
            _const_spec(w_down.shape),
        ],
        out_specs=pl.BlockSpec((1, tm, D_MODEL), lambda b, t: (b, t, 0)),
        out_shape=jax.ShapeDtypeStruct(h.shape, F32),
        scratch_shapes=[pltpu.VMEM((tm + 2 * HALO, D_MODEL), BF16),
                        pltpu.VMEM((tm, D_MODEL), F32)],
        compiler_params=_params(),
        name="conv_ffn",
    )(h, h, h, mod, norm_g, w_up, conv_w, conv_b, w_down)


CONF_ROWS = 64
CONF_LANES = 256


def _conf_kernel(hp_ref, h_ref, hn_ref, mod_ref, g_ref, win_ref, bin_ref, dw_ref, dwb_ref, lng_ref,
                 lnb_ref, wout_ref, bout_ref, o_ref, a_scr, z_scr, y_scr, *, tm, seq_len):
    t = pl.program_id(1)
    sh, sc, gate = _mod_rows(mod_ref, 0)
    _fill_modnorm(a_scr, hp_ref, h_ref, hn_ref, g_ref[0:1, :], sc, sh, tm)
    rows = tm + 2 * HALO

    a = a_scr[...]
    za = _mm(a, win_ref[0]) + bin_ref[0]
    zg = _mm(a, win_ref[1]) + bin_ref[1]
    pos = lax.broadcasted_iota(jnp.int32, (rows, 1), 0) + (t * tm - HALO)
    inside = jnp.logical_and(pos >= 0, pos < seq_len)
    z_scr[...] = jnp.where(inside, za * _sigmoid(zg), 0.0)

    half = (CONV_WIDTH - 1) // 2
    for r0 in range(0, tm, CONF_ROWS):
        for c0 in range(0, D_MODEL, CONF_LANES):
            acc = jnp.zeros((CONF_ROWS, CONF_LANES), F32)
            for k in range(CONV_WIDTH):
                start = r0 + HALO + k - half
                acc = acc + (z_scr[start:start + CONF_ROWS, c0:c0 + CONF_LANES]
                             * dw_ref[k:k + 1, c0:c0 + CONF_LANES])
            y_scr[r0:r0 + CONF_ROWS, c0:c0 + CONF_LANES] = acc + dwb_ref[:, c0:c0 + CONF_LANES]

    y = _layer_norm(y_scr[...], lng_ref[...], lnb_ref[...])
    y = (y * _sigmoid(y)).astype(BF16)
    out = _mm(y, wout_ref[...]) + bout_ref[...]
    o_ref[0] = h_ref[0] + gate * _rms(out, g_ref[1:2, :])


def _conf_call(h, mod, norm_g, w_in, b_in, dw_w, dw_b, ln_g, ln_b, w_out, b_out):
    bsz, seq_len, _ = h.shape
    tm = _tile_rows(seq_len)
    consts = [norm_g, w_in, b_in, dw_w, dw_b, ln_g, ln_b, w_out, b_out]
    return pl.pallas_call(
        functools.partial(_conf_kernel, tm=tm, seq_len=seq_len),
        grid=(bsz, seq_len // tm),
        in_specs=_hidden_specs(tm, seq_len) + [_mod_spec(mod)] + [_const_spec(c.shape) for c in consts],
        out_specs=pl.BlockSpec((1, tm, D_MODEL), lambda b, t: (b, t, 0)),
        out_shape=jax.ShapeDtypeStruct(h.shape, F32),
        scratch_shapes=[pltpu.VMEM((tm + 2 * HALO, D_MODEL), BF16),
                        pltpu.VMEM((tm + 2 * HALO, D_MODEL), F32),
                        pltpu.VMEM((tm, D_MODEL), F32)],
        compiler_params=_params(),
        name="conformer_conv",
    )(h, h, h, mod, *consts)


def _gmlp_kernel(h_ref, mod_ref, g_ref, wu_ref, bu_ref, wv_ref, bv_ref, lng_ref, lnb_ref, ws_ref,
                 bs_ref, wout_ref, o_ref, vn_scr, acc_scr, *, tm):
    sh, sc, gate = _mod_rows(mod_ref, 0)
    h = h_ref[0]
    a = (_rms(h, g_ref[0:1, :]) * (1.0 + sc) + sh).astype(BF16)

    v = _gelu_tanh(_mm(a, wv_ref[...]) + bv_ref[...])
    vn = _layer_norm(v, lng_ref[...], lnb_ref[...]).astype(BF16)
    for g in range(GMLP_GROUPS):
        vn_scr[g] = vn[:, g * GMLP_GROUP_DIM:(g + 1) * GMLP_GROUP_DIM]

    acc_scr[...] = jnp.zeros_like(acc_scr)
    for p in range(GMLP_NPAIR):
        u = _gelu_tanh(_mm(a, wu_ref[p]) + bu_ref[p])
        cols = []
        for g in (2 * p, 2 * p + 1):
            bias = bs_ref[:, g:g + 1]
            cols.append(jnp.concatenate(
                [_mm(ws_ref[g], vn_scr[g, c * GMLP_CHUNK:(c + 1) * GMLP_CHUNK, :]) + bias
                 for c in range(tm // GMLP_CHUNK)], axis=0))
        s = jnp.concatenate(cols, axis=1)
        acc_scr[...] += _mm((u * s).astype(BF16), wout_ref[p])

    o_ref[0] = h + gate * _rms(acc_scr[...], g_ref[1:2, :])


def _gmlp_call(h, mod, norm_g, w_u, b_u, w_v, b_v, ln_g, ln_b, w_s, b_s, w_out):
    bsz, seq_len, _ = h.shape
    tm = _tile_rows(seq_len)
    consts = [norm_g, w_u, b_u, w_v, b_v, ln_g, ln_b, w_s, b_s, w_out]
    return pl.pallas_call(
        functools.partial(_gmlp_kernel, tm=tm),
        grid=(bsz, seq_len // tm),
        in_specs=[pl.BlockSpec((1, tm, D_MODEL), lambda b, t: (b, t, 0)), _mod_spec(mod)]
        + [_const_spec(c.shape) for c in consts],
        out_specs=pl.BlockSpec((1, tm, D_MODEL), lambda b, t: (b, t, 0)),
        out_shape=jax.ShapeDtypeStruct(h.shape, F32),
        scratch_shapes=[pltpu.VMEM((GMLP_GROUPS, tm, GMLP_GROUP_DIM), BF16),
                        pltpu.VMEM((tm, D_MODEL), F32)],
        compiler_params=_params(),
        name="chunk_gmlp",
    )(h, mod, *consts)


def _expand_kv(x):
    low = lax.broadcasted_iota(jnp.int32, (1, LANES), 1) < HEAD_DIM
    tiles = []
    for c in range(KV_DIM // LANES):
        col = x[:, c * LANES:(c + 1) * LANES]
        swapped = pltpu.roll(col, HEAD_DIM, 1)
        tiles += [jnp.where(low, col, 0.0), jnp.where(low, 0.0, swapped),
                  jnp.where(low, swapped, 0.0), jnp.where(low, 0.0, col)]
    return jnp.concatenate(tiles, axis=1).astype(BF16)


def _rope(x, cos, sin_lo, sin_hi):
    quarter = HEAD_DIM // 4
    tiles = []
    for c in range(x.shape[1] // LANES):
        col = x[:, c * LANES:(c + 1) * LANES]
        tiles.append(col * cos + pltpu.roll(col, LANES - quarter, 1) * sin_lo
                     + pltpu.roll(col, quarter, 1) * sin_hi)
    return jnp.concatenate(tiles, axis=1)


def _qkv_kernel(h_ref, mod_ref, g_ref, w_ref, cos_ref, slo_ref, shi_ref, q_ref, kx_ref, vx_ref):
    sh, sc, _ = _mod_rows(mod_ref, 0)
    a = (_rms(h_ref[0], g_ref[0:1, :]) * (1.0 + sc) + sh).astype(BF16)
    qkv = _mm(a, w_ref[...])
    cos, slo, shi = cos_ref[...], slo_ref[...], shi_ref[...]
    q = _rope(qkv[:, :D_MODEL], cos, slo, shi) * (HEAD_DIM ** -0.5)
    q_ref[0] = q.astype(BF16)
    kx_ref[0] = _expand_kv(_rope(qkv[:, D_MODEL:D_MODEL + KV_DIM], cos, slo, shi))
    vx_ref[0] = _expand_kv(qkv[:, D_MODEL + KV_DIM:])


def _ctx_kv_kernel(h_ref, mod_ref, g_ref, w_ref, kx_ref, vx_ref):
    sh, sc, _ = _mod_rows(mod_ref, 0)
    a = (_rms(h_ref[0], g_ref[0:1, :]) * (1.0 + sc) + sh).astype(BF16)
    kv = _mm(a, w_ref[...])
    kx_ref[0] = _expand_kv(kv[:, :KV_DIM])
    vx_ref[0] = _expand_kv(kv[:, KV_DIM:])


def _qkv_call(h, mod, norm_g, w_qkv, cos, sin_lo, sin_hi):
    bsz, seq_len, _ = h.shape
    tm = _tile_rows(seq_len)
    tile = lambda b, t: (b, t, 0)
    table = pl.BlockSpec((tm, LANES), lambda b, t: (t, 0))
    return pl.pallas_call(
        _qkv_kernel,
        grid=(bsz, seq_len // tm),
        in_specs=[pl.BlockSpec((1, tm, D_MODEL), tile), _mod_spec(mod), _const_spec(norm_g.shape),
                  _const_spec(w_qkv.shape), table, table, table],
        out_specs=[pl.BlockSpec((1, tm, D_MODEL), tile), pl.BlockSpec((1, tm, KV_EXP), tile),
                   pl.BlockSpec((1, tm, KV_EXP), tile)],
        out_shape=[jax.ShapeDtypeStruct((bsz, seq_len, D_MODEL), BF16),
                   jax.ShapeDtypeStruct((bsz, seq_len, KV_EXP), BF16),
                   jax.ShapeDtypeStruct((bsz, seq_len, KV_EXP), BF16)],
        compiler_params=_params(),
        name="attn_qkv",
    )(h, mod, norm_g, w_qkv, cos, sin_lo, sin_hi)


def _ctx_kv_call(hc, mod, norm_g, w_kv):
    bsz, n_ctx, _ = hc.shape
    tile = lambda b, t: (b, t, 0)
    return pl.pallas_call(
        _ctx_kv_kernel,
        grid=(bsz, 1),
        in_specs=[pl.BlockSpec((1, n_ctx, D_MODEL), tile), _mod_spec(mod), _const_spec(norm_g.shape),
                  _const_spec(w_kv.shape)],
        out_specs=[pl.BlockSpec((1, n_ctx, KV_EXP), tile), pl.BlockSpec((1, n_ctx, KV_EXP), tile)],
        out_shape=[jax.ShapeDtypeStruct((bsz, n_ctx, KV_EXP), BF16),
                   jax.ShapeDtypeStruct((bsz, n_ctx, KV_EXP), BF16)],
        compiler_params=_params(),
        name="attn_ctx_kv",
    )(hc, mod, norm_g, w_kv)


def _attn_kernel(sink_ref, q_ref, kp_ref, kc_ref, kn_ref, vp_ref, vc_ref, vn_ref, kctx_ref, vctx_ref,
                 h_ref, mod_ref, g_ref, wo_ref, o_ref, *, n_ctx):
    n = pl.program_id(1)
    nb = pl.num_programs(1)
    _, _, gate = _mod_rows(mod_ref, 0)
    blk = ATTN_BLOCK
    n_keys = 3 * blk + n_ctx

    qi = lax.broadcasted_iota(jnp.int32, (2 * blk, 1), 0) % blk
    kj = lax.broadcasted_iota(jnp.int32, (1, n_keys), 1)
    far = 4 * blk
    in_prev = kj >= qi + jnp.where(n > 0, 0, far)
    in_next = kj - 2 * blk <= qi - jnp.where(n < nb - 1, 0, far)
    bias = jnp.where(kj < blk, jnp.where(in_prev, 0.0, -1e30),
                     jnp.where(jnp.logical_and(kj >= 2 * blk, kj < 3 * blk),
                               jnp.where(in_next, 0.0, -1e30), 0.0))
    first_head = lax.broadcasted_iota(jnp.int32, (2 * blk, 1), 0) < blk

    out_tiles = []
    for kv in range(N_KV_HEADS):
        base = kv * 2 * LANES
        qs = jnp.concatenate([q_ref[0, :, base:base + LANES], q_ref[0, :, base + LANES:base + 2 * LANES]], axis=0)
        o_sum = jnp.zeros((2 * blk, LANES), F32)
        for e in range(2):
            lo = base + e * LANES
            keys = jnp.concatenate([kp_ref[0, :, lo:lo + LANES], kc_ref[0, :, lo:lo + LANES],
                                    kn_ref[0, :, lo:lo + LANES], kctx_ref[0, :, lo:lo + LANES]], axis=0)
            vals = jnp.concatenate([vp_ref[0, :, lo:lo + LANES], vc_ref[0, :, lo:lo + LANES],
                                    vn_ref[0, :, lo:lo + LANES], vctx_ref[0, :, lo:lo + LANES]], axis=0)
            s = _mm_nt(qs, keys) + bias
            head = 4 * kv + e
            sink = jnp.where(first_head, sink_ref[head], sink_ref[head + 2])
            m = jnp.maximum(jnp.max(s, axis=-1, keepdims=True), sink)
            p = jnp.exp(s - m)
            inv = 1.0 / (jnp.sum(p, axis=-1, keepdims=True) + jnp.exp(sink - m))
            o_sum = o_sum + _mm(p.astype(BF16), vals) * inv
        out_tiles += [o_sum[:blk], o_sum[blk:]]

    o = jnp.concatenate(out_tiles, axis=1).astype(BF16)
    y = _mm(o, wo_ref[...])
    o_ref[0] = h_ref[0] + gate * _rms(y, g_ref[1:2, :])


def _attn_call(sink, q, kx, vx, kctx, vctx, h, mod, norm_g, w_o):
    bsz, seq_len, _ = h.shape
    n_ctx = kctx.shape[1]
    nb = seq_len // ATTN_BLOCK
    cur = lambda b, n: (b, n, 0)
    prev = lambda b, n: (b, jnp.maximum(n - 1, 0), 0)
    nxt = lambda b, n: (b, jnp.minimum(n + 1, nb - 1), 0)
    whole = lambda b, n: (b, 0, 0)
    kv_blk = (1, ATTN_BLOCK, KV_EXP)
    return pl.pallas_call(
        functools.partial(_attn_kernel, n_ctx=n_ctx),
        grid=(bsz, nb),
        in_specs=[
            pl.BlockSpec(memory_space=pltpu.SMEM),
            pl.BlockSpec((1, ATTN_BLOCK, D_MODEL), cur),
            pl.BlockSpec(kv_blk, prev), pl.BlockSpec(kv_blk, cur), pl.BlockSpec(kv_blk, nxt),
            pl.BlockSpec(kv_blk, prev), pl.BlockSpec(kv_blk, cur), pl.BlockSpec(kv_blk, nxt),
            pl.BlockSpec((1, n_ctx, KV_EXP), whole), pl.BlockSpec((1, n_ctx, KV_EXP), whole),
            pl.BlockSpec((1, ATTN_BLOCK, D_MODEL), cur),
            _mod_spec(mod), _const_spec(norm_g.shape), _const_spec(w_o.shape),
        ],
        out_specs=pl.BlockSpec((1, ATTN_BLOCK, D_MODEL), cur),
        out_shape=jax.ShapeDtypeStruct(h.shape, F32),
        compiler_params=_params(),
        name="attn_core",
    )(sink, q, kx, kx, kx, vx, vx, vx, kctx, vctx, h, mod, norm_g, w_o)


def _rope_tables(seq_len):
    rows = seq_len // GRID_W
    row = jnp.repeat(jnp.arange(rows), GRID_W).astype(F32)
    col = jnp.tile(jnp.arange(GRID_W), rows).astype(F32)
    axis_dim = HEAD_DIM // 2
    inv_freq = ROPE_BASE ** (-jnp.arange(0, axis_dim, 2, dtype=F32) / axis_dim)
    ang_r = row[:, None] * inv_freq[None, :]
    ang_c = col[:, None] * inv_freq[None, :]
    ang = jnp.concatenate([ang_r, ang_r, ang_c, ang_c], axis=-1)
    cos, sin = jnp.cos(ang), jnp.sin(ang)
    lower = (jnp.arange(HEAD_DIM) % (HEAD_DIM // 2)) < HEAD_DIM // 4
    sin_lo = jnp.where(lower[None, :], -sin, 0.0)
    sin_hi = jnp.where(lower[None, :], 0.0, sin)
    reps = LANES // HEAD_DIM
    return jnp.tile(cos, (1, reps)), jnp.tile(sin_lo, (1, reps)), jnp.tile(sin_hi, (1, reps))


def kernel(x, c, ctx, c_ctx, ada_w, ada_b, norm_g, ffn_w_up, ffn_conv_w, ffn_conv_b, ffn_w_down,
           cm_w_in, cm_b_in, cm_dw_w, cm_dw_b, cm_ln_g, cm_ln_b, cm_w_out, cm_b_out,
           attn_w_qkv, attn_sink, attn_w_o,
           gm_w_in, gm_b_in, gm_ln_g, gm_ln_b, gm_w_s, gm_b_s, gm_w_out):
    bsz, seq_len, d = x.shape

    cond_rows = -(-(bsz + 1) // 16) * 16
    cond = jnp.concatenate([c, c_ctx[None, :], jnp.zeros((cond_rows - bsz - 1, d), F32)], axis=0)
    mods = _ada_call(cond, ada_w, ada_b).reshape(DEPTH, cond_rows, 6, d)

    def ffn(hid, mod, i):
        w_up = ffn_w_up[i].reshape(d, 2, FFN_NCHUNK, FFN_CHUNK).transpose(1, 2, 0, 3).astype(BF16)
        conv_w = ffn_conv_w[i].reshape(3, 2, FFN_NCHUNK, FFN_CHUNK).transpose(1, 2, 0, 3)
        conv_b = ffn_conv_b[i].reshape(2, FFN_NCHUNK, 1, FFN_CHUNK)
        w_down = ffn_w_down[i].reshape(FFN_NCHUNK, FFN_CHUNK, d).astype(BF16)
        return _ffn_call(hid, mod, norm_g[i], w_up, conv_w, conv_b, w_down)

    def conformer(hid, mod, i, j):
        w_in = cm_w_in[j].reshape(d, 2, d).transpose(1, 0, 2).astype(BF16)
        return _conf_call(hid, mod, norm_g[i], w_in, cm_b_in[j].reshape(2, 1, d), cm_dw_w[j],
                          cm_dw_b[j][None, :], cm_ln_g[j][None, :], cm_ln_b[j][None, :],
                          cm_w_out[j].astype(BF16), cm_b_out[j][None, :])

    ctx_readers = [i for i in range(DEPTH) if i % N_MIXERS == 1]
    last_reader = max(ctx_readers) if ctx_readers else -1

    h, hc = x, ctx
    for i in range(DEPTH):
        kind, j = i % N_MIXERS, i // N_MIXERS
        ctx_full = i < last_reader
        mod = mods[i, :bsz]
        mod_c = mods[i, bsz:bsz + 1]

        if kind == 0:
            h = conformer(h, mod, i, j)
            if ctx_full:
                hc = conformer(hc, mod_c, i, j)
        elif kind == 1:
            cos, sin_lo, sin_hi = _rope_tables(seq_len)
            w_qkv = attn_w_qkv[j].astype(BF16)
            q, kx, vx = _qkv_call(h, mod, norm_g[i], w_qkv, cos, sin_lo, sin_hi)
            kctx, vctx = _ctx_kv_call(hc, mod_c, norm_g[i], w_qkv[:, d:])
            if ctx_full:
                raise NotImplementedError("context output of an attention layer is never consumed at this depth")
            h = _attn_call(attn_sink[j], q, kx, vx, kctx, vctx, h, mod, norm_g[i], attn_w_o[j].astype(BF16))
        else:
            w_in = gm_w_in[j]
            w_u = w_in[:, :GMLP_WIDTH].reshape(d, GMLP_NPAIR, GMLP_PAIR).transpose(1, 0, 2).astype(BF16)
            b_u = gm_b_in[j][:GMLP_WIDTH].reshape(GMLP_NPAIR, 1, GMLP_PAIR)
            h = _gmlp_call(h, mod, norm_g[i], w_u, b_u, w_in[:, GMLP_WIDTH:].astype(BF16),
                           gm_b_in[j][None, GMLP_WIDTH:], gm_ln_g[j][None, :], gm_ln_b[j][None, :],
                           gm_w_s[j].astype(BF16), gm_b_s[j].T,
                           gm_w_out[j].reshape(GMLP_NPAIR, GMLP_PAIR, d).astype(BF16))
            if ctx_full:
                raise NotImplementedError("context gMLP is never consumed at this depth")

        h = ffn(h, mod, i)
        if ctx_full:
            hc = ffn(hc, mod_c, i)
    return h
```

```python
import functools

import jax
import jax.numpy as jnp
from jax import lax
from jax.experimental import pallas as pl
from jax.experimental.pallas import tpu as pltpu

D_MODEL = 1024
DEPTH = 4
GRID_W = 64
N_MIXERS = 3
EPS = 1e-6
CONV_WIDTH = 31
HEAD_DIM = 64
N_Q_HEADS = D_MODEL // HEAD_DIM
N_KV_HEADS = N_Q_HEADS // 4
ATTN_BLOCK = 128
ROPE_BASE = 10000.0
GMLP_CHUNK = 128
GMLP_WIDTH = 2 * D_MODEL
GMLP_GROUP_DIM = 128
GMLP_GROUPS = GMLP_WIDTH // GMLP_GROUP_DIM
FFN_DIM = 2816

LANES = 128
HALO = 16
FFN_CHUNK = 256
FFN_NCHUNK = FFN_DIM // FFN_CHUNK
FFN_ROW_BLOCK = 128
SUBLANES = 8
GMLP_PAIR = 2 * GMLP_GROUP_DIM
GMLP_NPAIR = GMLP_WIDTH // GMLP_PAIR
KV_DIM = N_KV_HEADS * HEAD_DIM
KV_EXP = N_KV_HEADS * 2 * LANES
VMEM_LIMIT = 56 * 1024 * 1024

BF16 = jnp.bfloat16
F32 = jnp.float32


def _mm(a, b):
    return jnp.dot(a, b, preferred_element_type=F32)


def _mm_nt(a, b):
    return lax.dot_general(a, b, (((1,), (1,)), ((), ())), preferred_element_type=F32)


def _rms(x, g):
    return x * lax.rsqrt(jnp.mean(x * x, axis=-1, keepdims=True) + EPS) * g


def _layer_norm(x, g, b):
    mu = jnp.mean(x, axis=-1, keepdims=True)
    xc = x - mu
    var = jnp.mean(xc * xc, axis=-1, keepdims=True)
    return xc * lax.rsqrt(var + EPS) * g + b


def _sigmoid(x):
    return 1.0 / (1.0 + jnp.exp(-x))


def _gelu_tanh(x):
    return 0.5 * x * (1.0 + jnp.tanh(0.7978845608028654 * (x + 0.044715 * (x * x * x))))


def _mod_rows(mod_ref, first):
    return (mod_ref[0, first:first + 1, :], mod_ref[0, first + 1:first + 2, :],
            mod_ref[0, first + 2:first + 3, :])


def _fill_modnorm(a_scr, hp_ref, h_ref, hn_ref, g, sc, sh, tm):
    def f(x):
        return (_rms(x, g) * (1.0 + sc) + sh).astype(BF16)
    a_scr[0:HALO, :] = f(hp_ref[0])
    a_scr[HALO:HALO + tm, :] = f(h_ref[0])
    a_scr[HALO + tm:HALO + tm + HALO, :] = f(hn_ref[0])


def _ada_kernel(cond_ref, w_ref, b_ref, o_ref):
    c = cond_ref[...]
    s = (c * _sigmoid(c)).astype(BF16)
    o_ref[0] = _mm(s, w_ref[0].astype(BF16)) + b_ref[0]


def _ada_call(cond, ada_w, ada_b):
    rows = cond.shape[0]
    n_out = ada_w.shape[-1]
    tn = 1536
    return pl.pallas_call(
        _ada_kernel,
        grid=(DEPTH, n_out // tn),
        in_specs=[
            pl.BlockSpec((rows, D_MODEL), lambda i, n: (0, 0)),
            pl.BlockSpec((1, D_MODEL, tn), lambda i, n: (i, 0, n)),
            pl.BlockSpec((1, 1, tn), lambda i, n: (i, 0, n)),
        ],
        out_specs=pl.BlockSpec((1, rows, tn), lambda i, n: (i, 0, n)),
        out_shape=jax.ShapeDtypeStruct((DEPTH, rows, n_out), F32),
        compiler_params=pltpu.CompilerParams(
            dimension_semantics=("arbitrary", "arbitrary"), vmem_limit_bytes=VMEM_LIMIT),
        name="adaln",
    )(cond, ada_w, ada_b.reshape(DEPTH, 1, n_out))


def _const_spec(shape):
    nd = len(shape)
    return pl.BlockSpec(shape, lambda b, t: (0,) * nd, pipeline_mode=pl.Buffered(1))


def _hidden_specs(tm, seq_len):
    per = tm // HALO
    last = seq_len // HALO - 1
    return [
        pl.BlockSpec((1, HALO, D_MODEL), lambda b, t: (b, jnp.maximum(t * per - 1, 0), 0)),
        pl.BlockSpec((1, tm, D_MODEL), lambda b, t: (b, t, 0)),
        pl.BlockSpec((1, HALO, D_MODEL), lambda b, t: (b, jnp.minimum((t + 1) * per, last), 0)),
    ]


def _mod_spec(mod):
    if mod.shape[0] == 1:
        return pl.BlockSpec((1, 6, D_MODEL), lambda b, t: (0, 0, 0))
    return pl.BlockSpec((1, 6, D_MODEL), lambda b, t: (b, 0, 0))


def _params():
    return pltpu.CompilerParams(dimension_semantics=("arbitrary", "arbitrary"),
                                vmem_limit_bytes=VMEM_LIMIT)


def _tile_rows(seq_len):
    return min(512, seq_len)


def _ffn_kernel(hp_ref, h_ref, hn_ref, mod_ref, g_ref, wup_ref, cw_ref, cb_ref, wdn_ref, o_ref,
                a_scr, acc_scr, z_scr, act_scr, *, tm):
    t = pl.program_id(1)
    nt = pl.num_programs(1)
    sh, sc, gate = _mod_rows(mod_ref, 3)
    _fill_modnorm(a_scr, hp_ref, h_ref, hn_ref, g_ref[2:3, :], sc, sh, tm)

    @pl.when(t == 0)
    def _():
        a_scr[0:HALO, :] = jnp.zeros((HALO, D_MODEL), BF16)

    @pl.when(t == nt - 1)
    def _():
        a_scr[HALO + tm:HALO + tm + HALO, :] = jnp.zeros((HALO, D_MODEL), BF16)

    blk = FFN_ROW_BLOCK
    edge = SUBLANES

    def conv3(z_ref, r0, w, b):
        z = z_ref[r0 + HALO - edge:r0 + HALO + blk + edge, :]
        zm = pltpu.roll(z, 1, 0)[edge:edge + blk]
        zp = pltpu.roll(z, blk + 2 * edge - 1, 0)[edge:edge + blk]
        return zm * w[0:1] + z[edge:edge + blk] * w[1:2] + zp * w[2:3] + b

    def up(j, half):
        z_scr[j % 2, half] = _mm(a_scr[...], wup_ref[half, j])

    def gate_block(j, r0):
        yg = conv3(z_scr.at[j % 2, 0], r0, cw_ref[0, j], cb_ref[0, j])
        yv = conv3(z_scr.at[j % 2, 1], r0, cw_ref[1, j], cb_ref[1, j])
        act_scr[j % 2, r0:r0 + blk, :] = (yg * _sigmoid(yg) * yv).astype(BF16)

    up(0, 0)
    up(0, 1)
    row_blocks = list(range(0, tm, blk))
    split = len(row_blocks) // 2
    for j in range(FFN_NCHUNK):
        more = j + 1 < FFN_NCHUNK
        if more:
            up(j + 1, 0)
        for r0 in row_blocks[:split]:
            gate_block(j, r0)
        if more:
            up(j + 1, 1)
        for r0 in row_blocks[split:]:
            gate_block(j, r0)
        part = _mm(act_scr[j % 2], wdn_ref[j])
        if j == 0:
            acc_scr[...] = part
        else:
            acc_scr[...] += part
    o_ref[0] = h_ref[0] + gate * _rms(acc_scr[...], g_ref[3:4, :])


def _ffn_call(h, mod, norm_g, w_up, conv_w, conv_b, w_down):
    bsz, seq_len, _ = h.shape
    tm = _tile_rows(seq_len)
    return pl.pallas_call(
        functools.partial(_ffn_kernel, tm=tm),
        grid=(bsz, seq_len // tm),
        in_specs=_hidden_specs(tm, seq_len) + [
            _mod_spec(mod),
            _const_spec(norm_g.shape),
            _const_spec(w_up.shape),
            _const_spec(conv_w.shape),
            _const_spec(conv_b.shape),
            _const_spec(w_down.shape),
        ],
        out_specs=pl.BlockSpec((1, tm, D_MODEL), lambda b, t: (b, t, 0)),
        out_shape=jax.ShapeDtypeStruct(h.shape, F32),
        scratch_shapes=[pltpu.VMEM((tm + 2 * HALO, D_MODEL), BF16),
                        pltpu.VMEM((tm, D_MODEL), F32),
                        pltpu.VMEM((2, 2, tm + 2 * HALO, FFN_CHUNK), F32),
                        pltpu.VMEM((2, tm, FFN_CHUNK), BF16)],
        compiler_params=_params(),
        name="conv_ffn",
    )(h, h, h, mod, norm_g, w_up, conv_w, conv_b, w_down)


CONF_ACCS = 17


def _conf_kernel(hp_ref, h_ref, hn_ref, mod_ref, g_ref, win_ref, bin_ref, dw_ref, dwb_ref, lng_ref,
                 lnb_ref, wout_ref, bout_ref, o_ref, a_scr, zs_scr, ys_scr, *, tm, seq_len):
    t = pl.program_id(1)
    sh, sc, gate = _mod_rows(mod_ref, 0)
    _fill_modnorm(a_scr, hp_ref, h_ref, hn_ref, g_ref[0:1, :], sc, sh, tm)
    rows = tm + 2 * HALO
    n_slab = D_MODEL // LANES

    a = a_scr[...]
    za = _mm(a, win_ref[0]) + bin_ref[0]
    zg = _mm(a, win_ref[1]) + bin_ref[1]
    pos = lax.broadcasted_iota(jnp.int32, (rows, 1), 0) + (t * tm - HALO)
    inside = jnp.logical_and(pos >= 0, pos < seq_len)
    z = jnp.where(inside, za * _sigmoid(zg), 0.0)
    pad = jnp.zeros((HALO, LANES), F32)
    for q in range(n_slab):
        zs_scr[q, 0:HALO, :] = pad
        zs_scr[q, HALO:HALO + rows, :] = z[:, q * LANES:(q + 1) * LANES]
        zs_scr[q, HALO + rows:HALO + rows + HALO, :] = pad

    seg = rows // SUBLANES
    half = (CONV_WIDTH - 1) // 2
    for q in range(n_slab):
        lanes = slice(q * LANES, (q + 1) * LANES)
        taps = [jnp.broadcast_to(dw_ref[k:k + 1, lanes], (SUBLANES, LANES)) for k in range(CONV_WIDTH)]
        bias = jnp.broadcast_to(dwb_ref[:, lanes], (SUBLANES, LANES))
        for j0 in range(0, seg, CONF_ACCS):
            js = range(j0, min(j0 + CONF_ACCS, seg))
            acc = {j: bias for j in js}
            for shift in range(js[0], js[-1] + CONV_WIDTH):
                v = zs_scr[q, pl.ds(shift + HALO - half, SUBLANES, stride=seg), :]
                for j in js:
                    if 0 <= shift - j < CONV_WIDTH:
                        acc[j] = acc[j] + v * taps[shift - j]
            for j in js:
                ys_scr[q, pl.ds(j, SUBLANES, stride=seg), :] = acc[j]

    y = jnp.concatenate([ys_scr[q, HALO:HALO + tm, :] for q in range(n_slab)], axis=1)
    y = _layer_norm(y, lng_ref[...], lnb_ref[...])
    y = (y * _sigmoid(y)).astype(BF16)
    out = _mm(y, wout_ref[...]) + bout_ref[...]
    o_ref[0] = h_ref[0] + gate * _rms(out, g_ref[1:2, :])


def _conf_call(h, mod, norm_g, w_in, b_in, dw_w, dw_b, ln_g, ln_b, w_out, b_out):
    bsz, seq_len, _ = h.shape
    tm = _tile_rows(seq_len)
    consts = [norm_g, w_in, b_in, dw_w, dw_b, ln_g, ln_b, w_out, b_out]
    return pl.pallas_call(
        functools.partial(_conf_kernel, tm=tm, seq_len=seq_len),
        grid=(bsz, seq_len // tm),
        in_specs=_hidden_specs(tm, seq_len) + [_mod_spec(mod)] + [_const_spec(c.shape) for c in consts],
        out_specs=pl.BlockSpec((1, tm, D_MODEL), lambda b, t: (b, t, 0)),
        out_shape=jax.ShapeDtypeStruct(h.shape, F32),
        scratch_shapes=[pltpu.VMEM((tm + 2 * HALO, D_MODEL), BF16),
                        pltpu.VMEM((D_MODEL // LANES, tm + 4 * HALO, LANES), F32),
                        pltpu.VMEM((D_MODEL // LANES, tm + 2 * HALO, LANES), F32)],
        compiler_params=_params(),
        name="conformer_conv",
    )(h, h, h, mod, *consts)


def _gmlp_kernel(h_ref, mod_ref, g_ref, wu_ref, bu_ref, wv_ref, bv_ref, lng_ref, lnb_ref, ws_ref,
                 bs_ref, wout_ref, o_ref, vn_scr, acc_scr, *, tm):
    sh, sc, gate = _mod_rows(mod_ref, 0)
    h = h_ref[0]
    a = (_rms(h, g_ref[0:1, :]) * (1.0 + sc) + sh).astype(BF16)

    v = _gelu_tanh(_mm(a, wv_ref[...]) + bv_ref[...])
    vn = _layer_norm(v, lng_ref[...], lnb_ref[...]).astype(BF16)
    for g in range(GMLP_GROUPS):
        vn_scr[g] = vn[:, g * GMLP_GROUP_DIM:(g + 1) * GMLP_GROUP_DIM]

    acc_scr[...] = jnp.zeros_like(acc_scr)
    for p in range(GMLP_NPAIR):
        u = _gelu_tanh(_mm(a, wu_ref[p]) + bu_ref[p])
        cols = []
        for g in (2 * p, 2 * p + 1):
            bias = bs_ref[:, g:g + 1]
            cols.append(jnp.concatenate(
                [_mm(ws_ref[g], vn_scr[g, c * GMLP_CHUNK:(c + 1) * GMLP_CHUNK, :]) + bias
                 for c in range(tm // GMLP_CHUNK)], axis=0))
        s = jnp.concatenate(cols, axis=1)
        acc_scr[...] += _mm((u * s).astype(BF16), wout_ref[p])

    o_ref[0] = h + gate * _rms(acc_scr[...], g_ref[1:2, :])


def _gmlp_call(h, mod, norm_g, w_u, b_u, w_v, b_v, ln_g, ln_b, w_s, b_s, w_out):
    bsz, seq_len, _ = h.shape
    tm = _tile_rows(seq_len)
    consts = [norm_g, w_u, b_u, w_v, b_v, ln_g, ln_b, w_s, b_s, w_out]
    return pl.pallas_call(
        functools.partial(_gmlp_kernel, tm=tm),
        grid=(bsz, seq_len // tm),
        in_specs=[pl.BlockSpec((1, tm, D_MODEL), lambda b, t: (b, t, 0)), _mod_spec(mod)]
        + [_const_spec(c.shape) for c in consts],
        out_specs=pl.BlockSpec((1, tm, D_MODEL), lambda b, t: (b, t, 0)),
        out_shape=jax.ShapeDtypeStruct(h.shape, F32),
        scratch_shapes=[pltpu.VMEM((GMLP_GROUPS, tm, GMLP_GROUP_DIM), BF16),
                        pltpu.VMEM((tm, D_MODEL), F32)],
        compiler_params=_params(),
        name="chunk_gmlp",
    )(h, mod, *consts)


def _expand_kv(x):
    low = lax.broadcasted_iota(jnp.int32, (1, LANES), 1) < HEAD_DIM
    tiles = []
    for c in range(KV_DIM // LANES):
        col = x[:, c * LANES:(c + 1) * LANES]
        swapped = pltpu.roll(col, HEAD_DIM, 1)
        tiles += [jnp.where(low, col, 0.0), jnp.where(low, 0.0, swapped),
                  jnp.where(low, swapped, 0.0), jnp.where(low, 0.0, col)]
    return jnp.concatenate(tiles, axis=1).astype(BF16)


def _rope(x, cos, sin_lo, sin_hi):
    quarter = HEAD_DIM // 4
    tiles = []
    for c in range(x.shape[1] // LANES):
        col = x[:, c * LANES:(c + 1) * LANES]
        tiles.append(col * cos + pltpu.roll(col, LANES - quarter, 1) * sin_lo
                     + pltpu.roll(col, quarter, 1) * sin_hi)
    return jnp.concatenate(tiles, axis=1)


def _qkv_kernel(h_ref, mod_ref, g_ref, w_ref, cos_ref, slo_ref, shi_ref, q_ref, kx_ref, vx_ref):
    sh, sc, _ = _mod_rows(mod_ref, 0)
    a = (_rms(h_ref[0], g_ref[0:1, :]) * (1.0 + sc) + sh).astype(BF16)
    qkv = _mm(a, w_ref[...])
    cos, slo, shi = cos_ref[...], slo_ref[...], shi_ref[...]
    q = _rope(qkv[:, :D_MODEL], cos, slo, shi) * (HEAD_DIM ** -0.5)
    q_ref[0] = q.astype(BF16)
    kx_ref[0] = _expand_kv(_rope(qkv[:, D_MODEL:D_MODEL + KV_DIM], cos, slo, shi))
    vx_ref[0] = _expand_kv(qkv[:, D_MODEL + KV_DIM:])


def _ctx_kv_kernel(h_ref, mod_ref, g_ref, w_ref, kx_ref, vx_ref):
    sh, sc, _ = _mod_rows(mod_ref, 0)
    a = (_rms(h_ref[0], g_ref[0:1, :]) * (1.0 + sc) + sh).astype(BF16)
    kv = _mm(a, w_ref[...])
    kx_ref[0] = _expand_kv(kv[:, :KV_DIM])
    vx_ref[0] = _expand_kv(kv[:, KV_DIM:])


def _qkv_call(h, mod, norm_g, w_qkv, cos, sin_lo, sin_hi):
    bsz, seq_len, _ = h.shape
    tm = _tile_rows(seq_len)
    tile = lambda b, t: (b, t, 0)
    table = pl.BlockSpec((tm, LANES), lambda b, t: (t, 0))
    return pl.pallas_call(
        _qkv_kernel,
        grid=(bsz, seq_len // tm),
        in_specs=[pl.BlockSpec((1, tm, D_MODEL), tile), _mod_spec(mod), _const_spec(norm_g.shape),
                  _const_spec(w_qkv.shape), table, table, table],
        out_specs=[pl.BlockSpec((1, tm, D_MODEL), tile), pl.BlockSpec((1, tm, KV_EXP), tile),
                   pl.BlockSpec((1, tm, KV_EXP), tile)],
        out_shape=[jax.ShapeDtypeStruct((bsz, seq_len, D_MODEL), BF16),
                   jax.ShapeDtypeStruct((bsz, seq_len, KV_EXP), BF16),
                   jax.ShapeDtypeStruct((bsz, seq_len, KV_EXP), BF16)],
        compiler_params=_params(),
        name="attn_qkv",
    )(h, mod, norm_g, w_qkv, cos, sin_lo, sin_hi)


def _ctx_kv_call(hc, mod, norm_g, w_kv):
    bsz, n_ctx, _ = hc.shape
    tile = lambda b, t: (b, t, 0)
    return pl.pallas_call(
        _ctx_kv_kernel,
        grid=(bsz, 1),
        in_specs=[pl.BlockSpec((1, n_ctx, D_MODEL), tile), _mod_spec(mod), _const_spec(norm_g.shape),
                  _const_spec(w_kv.shape)],
        out_specs=[pl.BlockSpec((1, n_ctx, KV_EXP), tile), pl.BlockSpec((1, n_ctx, KV_EXP), tile)],
        out_shape=[jax.ShapeDtypeStruct((bsz, n_ctx, KV_EXP), BF16),
                   jax.ShapeDtypeStruct((bsz, n_ctx, KV_EXP), BF16)],
        compiler_params=_params(),
        name="attn_ctx_kv",
    )(hc, mod, norm_g, w_kv)


def _attn_kernel(sink_ref, q_ref, kp_ref, kc_ref, kn_ref, vp_ref, vc_ref, vn_ref, kctx_ref, vctx_ref,
                 h_ref, mod_ref, g_ref, wo_ref, o_ref, *, n_ctx):
    n = pl.program_id(1)
    nb = pl.num_programs(1)
    _, _, gate = _mod_rows(mod_ref, 0)
    blk = ATTN_BLOCK
    n_keys = 3 * blk + n_ctx

    qi = lax.broadcasted_iota(jnp.int32, (2 * blk, 1), 0) % blk
    kj = lax.broadcasted_iota(jnp.int32, (1, n_keys), 1)
    far = 4 * blk
    in_prev = kj >= qi + jnp.where(n > 0, 0, far)
    in_next = kj - 2 * blk <= qi - jnp.where(n < nb - 1, 0, far)
    bias = jnp.where(kj < blk, jnp.where(in_prev, 0.0, -1e30),
                     jnp.where(jnp.logical_and(kj >= 2 * blk, kj < 3 * blk),
                               jnp.where(in_next, 0.0, -1e30), 0.0))
    first_head = lax.broadcasted_iota(jnp.int32, (2 * blk, 1), 0) < blk

    out_tiles = []
    for kv in range(N_KV_HEADS):
        base = kv * 2 * LANES
        qs = jnp.concatenate([q_ref[0, :, base:base + LANES], q_ref[0, :, base + LANES:base + 2 * LANES]], axis=0)
        o_sum = jnp.zeros((2 * blk, LANES), F32)
        for e in range(2):
            lo = base + e * LANES
            keys = jnp.concatenate([kp_ref[0, :, lo:lo + LANES], kc_ref[0, :, lo:lo + LANES],
                                    kn_ref[0, :, lo:lo + LANES], kctx_ref[0, :, lo:lo + LANES]], axis=0)
            vals = jnp.concatenate([vp_ref[0, :, lo:lo + LANES], vc_ref[0, :, lo:lo + LANES],
                                    vn_ref[0, :, lo:lo + LANES], vctx_ref[0, :, lo:lo + LANES]], axis=0)
            s = _mm_nt(qs, keys) + bias
            head = 4 * kv + e
            sink = jnp.where(first_head, sink_ref[head], sink_ref[head + 2])
            m = jnp.maximum(jnp.max(s, axis=-1, keepdims=True), sink)
            p = jnp.exp(s - m)
            inv = 1.0 / (jnp.sum(p, axis=-1, keepdims=True) + jnp.exp(sink - m))
            o_sum = o_sum + _mm(p.astype(BF16), vals) * inv
        out_tiles += [o_sum[:blk], o_sum[blk:]]

    o = jnp.concatenate(out_tiles, axis=1).astype(BF16)
    y = _mm(o, wo_ref[...])
    o_ref[0] = h_ref[0] + gate * _rms(y, g_ref[1:2, :])


def _attn_call(sink, q, kx, vx, kctx, vctx, h, mod, norm_g, w_o):
    bsz, seq_len, _ = h.shape
    n_ctx = kctx.shape[1]
    nb = seq_len // ATTN_BLOCK
    cur = lambda b, n: (b, n, 0)
    prev = lambda b, n: (b, jnp.maximum(n - 1, 0), 0)
    nxt = lambda b, n: (b, jnp.minimum(n + 1, nb - 1), 0)
    whole = lambda b, n: (b, 0, 0)
    kv_blk = (1, ATTN_BLOCK, KV_EXP)
    return pl.pallas_call(
        functools.partial(_attn_kernel, n_ctx=n_ctx),
        grid=(bsz, nb),
        in_specs=[
            pl.BlockSpec(memory_space=pltpu.SMEM),
            pl.BlockSpec((1, ATTN_BLOCK, D_MODEL), cur),
            pl.BlockSpec(kv_blk, prev), pl.BlockSpec(kv_blk, cur), pl.BlockSpec(kv_blk, nxt),
            pl.BlockSpec(kv_blk, prev), pl.BlockSpec(kv_blk, cur), pl.BlockSpec(kv_blk, nxt),
            pl.BlockSpec((1, n_ctx, KV_EXP), whole), pl.BlockSpec((1, n_ctx, KV_EXP), whole),
            pl.BlockSpec((1, ATTN_BLOCK, D_MODEL), cur),
            _mod_spec(mod), _const_spec(norm_g.shape), _const_spec(w_o.shape),
        ],
        out_specs=pl.BlockSpec((1, ATTN_BLOCK, D_MODEL), cur),
        out_shape=jax.ShapeDtypeStruct(h.shape, F32),
        compiler_params=_params(),
        name="attn_core",
    )(sink, q, kx, kx, kx, vx, vx, vx, kctx, vctx, h, mod, norm_g, w_o)


def _rope_tables(seq_len):
    rows = seq_len // GRID_W
    row = jnp.repeat(jnp.arange(rows), GRID_W).astype(F32)
    col = jnp.tile(jnp.arange(GRID_W), rows).astype(F32)
    axis_dim = HEAD_DIM // 2
    inv_freq = ROPE_BASE ** (-jnp.arange(0, axis_dim, 2, dtype=F32) / axis_dim)
    ang_r = row[:, None] * inv_freq[None, :]
    ang_c = col[:, None] * inv_freq[None, :]
    ang = jnp.concatenate([ang_r, ang_r, ang_c, ang_c], axis=-1)
    cos, sin = jnp.cos(ang), jnp.sin(ang)
    lower = (jnp.arange(HEAD_DIM) % (HEAD_DIM // 2)) < HEAD_DIM // 4
    sin_lo = jnp.where(lower[None, :], -sin, 0.0)
    sin_hi = jnp.where(lower[None, :], 0.0, sin)
    reps = LANES // HEAD_DIM
    return jnp.tile(cos, (1, reps)), jnp.tile(sin_lo, (1, reps)), jnp.tile(sin_hi, (1, reps))


def kernel(x, c, ctx, c_ctx, ada_w, ada_b, norm_g, ffn_w_up, ffn_conv_w, ffn_conv_b, ffn_w_down,
           cm_w_in, cm_b_in, cm_dw_w, cm_dw_b, cm_ln_g, cm_ln_b, cm_w_out, cm_b_out,
           attn_w_qkv, attn_sink, attn_w_o,
           gm_w_in, gm_b_in, gm_ln_g, gm_ln_b, gm_w_s, gm_b_s, gm_w_out):
    bsz, seq_len, d = x.shape

    cond_rows = -(-(bsz + 1) // 16) * 16
    cond = jnp.concatenate([c, c_ctx[None, :], jnp.zeros((cond_rows - bsz - 1, d), F32)], axis=0)
    mods = _ada_call(cond, ada_w, ada_b).reshape(DEPTH, cond_rows, 6, d)

    def ffn(hid, mod, i):
        w_up = ffn_w_up[i].reshape(d, 2, FFN_NCHUNK, FFN_CHUNK).transpose(1, 2, 0, 3).astype(BF16)
        conv_w = ffn_conv_w[i].reshape(3, 2, FFN_NCHUNK, FFN_CHUNK).transpose(1, 2, 0, 3)
        conv_b = ffn_conv_b[i].reshape(2, FFN_NCHUNK, 1, FFN_CHUNK)
        w_down = ffn_w_down[i].reshape(FFN_NCHUNK, FFN_CHUNK, d).astype(BF16)
        return _ffn_call(hid, mod, norm_g[i], w_up, conv_w, conv_b, w_down)

    def conformer(hid, mod, i, j):
        w_in = cm_w_in[j].reshape(d, 2, d).transpose(1, 0, 2).astype(BF16)
        return _conf_call(hid, mod, norm_g[i], w_in, cm_b_in[j].reshape(2, 1, d), cm_dw_w[j],
                          cm_dw_b[j][None, :], cm_ln_g[j][None, :], cm_ln_b[j][None, :],
                          cm_w_out[j].astype(BF16), cm_b_out[j][None, :])

    ctx_readers = [i for i in range(DEPTH) if i % N_MIXERS == 1]
    last_reader = max(ctx_readers) if ctx_readers else -1

    h, hc = x, ctx
    for i in range(DEPTH):
        kind, j = i % N_MIXERS, i // N_MIXERS
        ctx_full = i < last_reader
        mod = mods[i, :bsz]
        mod_c = mods[i, bsz:bsz + 1]

        if kind == 0:
            h = conformer(h, mod, i, j)
            if ctx_full:
                hc = conformer(hc, mod_c, i, j)
        elif kind == 1:
            cos, sin_lo, sin_hi = _rope_tables(seq_len)
            w_qkv = attn_w_qkv[j].astype(BF16)
            q, kx, vx = _qkv_call(h, mod, norm_g[i], w_qkv, cos, sin_lo, sin_hi)
            kctx, vctx = _ctx_kv_call(hc, mod_c, norm_g[i], w_qkv[:, d:])
            if ctx_full:
                raise NotImplementedError("context output of an attention layer is never consumed at this depth")
            h = _attn_call(attn_sink[j], q, kx, vx, kctx, vctx, h, mod, norm_g[i], attn_w_o[j].astype(BF16))
        else:
            w_in = gm_w_in[j]
            w_u = w_in[:, :GMLP_WIDTH].reshape(d, GMLP_NPAIR, GMLP_PAIR).transpose(1, 0, 2).astype(BF16)
            b_u = gm_b_in[j][:GMLP_WIDTH].reshape(GMLP_NPAIR, 1, GMLP_PAIR)
            h = _gmlp_call(h, mod, norm_g[i], w_u, b_u, w_in[:, GMLP_WIDTH:].astype(BF16),
                           gm_b_in[j][None, GMLP_WIDTH:], gm_ln_g[j][None, :], gm_ln_b[j][None, :],
                           gm_w_s[j].astype(BF16), gm_b_s[j].T,
                           gm_w_out[j].reshape(GMLP_NPAIR, GMLP_PAIR, d).astype(BF16))
            if ctx_full:
                raise NotImplementedError("context gMLP is never consumed at this depth")

        h = ffn(h, mod, i)
        if ctx_full:
            hc = ffn(hc, mod_c, i)
    return h
```
